```python
import math
import jax, jax.numpy as jnp
from jax import lax
import numpy as np

D_MODEL = 2048
BATCH = 2
SEQ = 4096
DEPTH = 4
DEC_BATCH = 128
DEC_SEQ = 1
PAST_LEN = 8192
PAGE_SIZE = 128

EPS = 1e-6
N_BRANCH = 4
GLA_HEADS = 4
GLA_DK = 64
GLA_DV = 128
GLA_WIDTH = GLA_HEADS * GLA_DV
GLA_LOWRANK = 16
GLA_TAU = 16.0
GLA_CHUNK = 64
SSM_WIDTH = 512
SSM_GROUP_CH = 16
SSM_GROUPS = SSM_WIDTH // SSM_GROUP_CH
SSM_STATE = 64
SGU_WIDTH = 512
SGU_HEADS = 4
SGU_HEAD_W = SGU_WIDTH // SGU_HEADS
SGU_CHUNK = 128
MLA_HEADS = 4
MLA_Q_LORA = 384
MLA_KV_LORA = 128
MLA_NOPE = 128
MLA_ROPE = 64
MLA_V = 128
MLA_WIDTH = MLA_HEADS * MLA_V
MLA_SCALE = (MLA_NOPE + MLA_ROPE) ** -0.5
ROPE_THETA = 10000.0
ATTN_BLOCK = 128
BRANCH_W = 512
D_FF = 5632
CONV_W = 3
IN_SIZES = (N_BRANCH * D_MODEL, GLA_HEADS * GLA_DK, GLA_HEADS * GLA_DK, GLA_WIDTH, GLA_WIDTH, GLA_LOWRANK,
            SSM_WIDTH, 2 * SGU_WIDTH, MLA_Q_LORA, MLA_KV_LORA, MLA_ROPE)
N_IN = sum(IN_SIZES)

kernel_name = 'hybrid_gated_branch_decoder_step'


def rmsnorm(x, g):
    xf = x.astype(jnp.float32)
    y = xf * lax.rsqrt(jnp.mean(xf * xf, axis=-1, keepdims=True) + EPS)
    return (y * g.astype(jnp.float32)).astype(x.dtype)


def layernorm(x, g, b):
    xf = x.astype(jnp.float32)
    xc = xf - jnp.mean(xf, axis=-1, keepdims=True)
    y = xc * lax.rsqrt(jnp.mean(xc * xc, axis=-1, keepdims=True) + EPS)
    return (y * g.astype(jnp.float32) + b.astype(jnp.float32)).astype(x.dtype)


def split_cols(p):
    outs, start = [], 0
    for n in IN_SIZES:
        outs.append(p[..., start:start + n])
        start += n
    return outs


def rope_cos_sin(pos):
    half = MLA_ROPE // 2
    inv = ROPE_THETA ** (-jnp.arange(half, dtype=jnp.float32) / half)
    ang = pos.astype(jnp.float32)[:, None] * inv[None, :]
    return jnp.cos(ang), jnp.sin(ang)


def apply_rope(x, cos, sin):
    x1, x2 = jnp.split(x.astype(jnp.float32), 2, axis=-1)
    return jnp.concatenate([x1 * cos - x2 * sin, x2 * cos + x1 * sin], axis=-1).astype(x.dtype)


def gla_scan(q, k, v, log_a, s0):
    b, s, h, dk = q.shape
    c = min(GLA_CHUNK, s)
    n = -(-s // c)
    pad = n * c - s
    if pad:
        pw = ((0, 0), (0, pad), (0, 0), (0, 0))
        q, k, v, log_a = [jnp.pad(t, pw) for t in (q, k, v, log_a)]
    f32 = jnp.float32
    qc = q.astype(f32).reshape(b, n, c, h, dk)
    kc = k.astype(f32).reshape(b, n, c, h, dk)
    vc = v.astype(f32).reshape(b, n, c, h, -1)
    cum = jnp.cumsum(log_a.astype(f32).reshape(b, n, c, h, dk), axis=2)
    last = cum[:, :, -1]
    q_dec = qc * jnp.exp(cum)
    k_dec = kc * jnp.exp(-cum)
    k_tail = kc * jnp.exp(last[:, :, None] - cum)
    att = jnp.einsum('bnihd,bnjhd->bnhij', q_dec, k_dec)
    att = jnp.where(jnp.tril(jnp.ones((c, c), bool)), att, 0.0)
    o_intra = jnp.einsum('bnhij,bnjhe->bnihe', att, vc)
    upd = jnp.einsum('bnjhd,bnjhe->bnhde', k_tail, vc)

    def step(state, xs):
        q_d, dec, u = xs
        o = jnp.einsum('bihd,bhde->bihe', q_d, state)
        return state * jnp.exp(dec)[..., None] + u, o

    s_fin, o_inter = lax.scan(step, s0.astype(f32),
                              (jnp.moveaxis(q_dec, 1, 0), jnp.moveaxis(last, 1, 0), jnp.moveaxis(upd, 1, 0)))
    o = o_intra + jnp.moveaxis(o_inter, 0, 1)
    return o.reshape(b, n * c, h, -1)[:, :s], s_fin


def gla_branch(q, k, v, og, a_lr, w_a_up, b_a, g_norm, s0):
    b, s, _ = q.shape
    q = q.reshape(b, s, GLA_HEADS, GLA_DK) * (GLA_DK ** -0.5)
    k = k.reshape(b, s, GLA_HEADS, GLA_DK)
    v = v.reshape(b, s, GLA_HEADS, GLA_DV)
    log_a = jax.nn.log_sigmoid((a_lr @ w_a_up + b_a).astype(jnp.float32)) / GLA_TAU
    log_a = log_a.reshape(b, s, GLA_HEADS, GLA_DK)
    o, s_fin = gla_scan(q, k, v, log_a, s0)
    o = rmsnorm(o, g_norm) * jax.nn.silu(og.astype(jnp.float32)).reshape(b, s, GLA_HEADS, GLA_DV)
    return o.reshape(b, s, GLA_WIDTH).astype(og.dtype), s_fin


def cplx_affine_combine(e1, e2):
    a1r, a1i, b1r, b1i = e1
    a2r, a2i, b2r, b2i = e2
    return (a2r * a1r - a2i * a1i, a2r * a1i + a2i * a1r,
            a2r * b1r - a2i * b1i + b2r, a2r * b1i + a2i * b1r + b2i)


def ssm_branch(u, a_re, a_im, b_re, b_im, c_re, c_im, d, log_dt, w_glu, b_glu, x0):
    f32 = jnp.float32
    bsz, s, _ = u.shape
    uf = u.astype(f32)
    ug = uf.reshape(bsz, s, SSM_GROUPS, SSM_GROUP_CH)
    dt = jnp.exp(log_dt.astype(f32))[:, None]
    ar, ai = a_re.astype(f32), a_im.astype(f32)
    mag = jnp.exp(dt * ar)
    abr, abi = mag * jnp.cos(dt * ai), mag * jnp.sin(dt * ai)
    den = ar * ar + ai * ai
    fr = ((abr - 1.0) * ar + abi * ai) / den
    fi = (abi * ar - (abr - 1.0) * ai) / den
    bur = jnp.einsum('gnc,bsgc->bsgn', b_re.astype(f32), ug)
    bui = jnp.einsum('gnc,bsgc->bsgn', b_im.astype(f32), ug)
    xr = fr * bur - fi * bui
    xi = fr * bui + fi * bur
    pr, pim, hr, hi = lax.associative_scan(
        cplx_affine_combine,
        (jnp.broadcast_to(abr, xr.shape), jnp.broadcast_to(abi, xr.shape), xr, xi), axis=1)
    x0r = x0[..., 0].astype(f32)[:, None]
    x0i = x0[..., 1].astype(f32)[:, None]
    hr = hr + pr * x0r - pim * x0i
    hi = hi + pr * x0i + pim * x0r
    y = jnp.einsum('gcn,bsgn->bsgc', c_re.astype(f32), hr) - jnp.einsum('gcn,bsgn->bsgc', c_im.astype(f32), hi)
    y = jax.nn.gelu(y.reshape(bsz, s, SSM_WIDTH) + d.astype(f32) * uf)
    out = y * jax.nn.sigmoid(y @ w_glu.astype(f32) + b_glu.astype(f32))
    return out.astype(u.dtype), jnp.stack([hr[:, -1], hi[:, -1]], axis=-1)


def sgu_branch(z, ln_g, ln_b, w_s, b_s):
    b, s, _ = z.shape
    u, v = jnp.split(jax.nn.gelu(z), 2, axis=-1)
    v = layernorm(v, ln_g, ln_b)
    c = min(SGU_CHUNK, s)
    n = -(-s // c)
    v_state = v[:, (n - 1) * c:]
    vp = jnp.pad(v, ((0, 0), (0, n * c - s), (0, 0))) if n * c > s else v
    vc = vp.reshape(b, n, c, SGU_HEADS, SGU_HEAD_W)
    w = jnp.tril(w_s[:, :c, :c])
    mixed = jnp.einsum('hij,bnjhe->bnihe', w, vc) + jnp.transpose(b_s[:, :c])[:, :, None]
    out = u * mixed.reshape(b, n * c, SGU_WIDTH)[:, :s].astype(u.dtype)
    return out, v_state


def mla_scores(q_lat, q_rope, k_lat, k_rope):
    sc = jnp.einsum('bqhc,bkc->bhqk', q_lat, k_lat, preferred_element_type=jnp.float32)
    sc = sc + jnp.einsum('bqhr,bkr->bhqk', q_rope, k_rope, preferred_element_type=jnp.float32)
    return sc * MLA_SCALE


def mla_attend_prompt(q_lat, q_rope, lat, k_rope):
    b, s, h, c = q_lat.shape
    blk = min(ATTN_BLOCK, s)
    nb = s // blk
    k_pos = jnp.arange(s)

    def block(args):
        i, ql, qr = args
        sc = mla_scores(ql, qr, lat, k_rope)
        q_pos = i * blk + jnp.arange(blk)
        sc = jnp.where(k_pos[None, :] <= q_pos[:, None], sc, -jnp.inf)
        p = jax.nn.softmax(sc, axis=-1).astype(lat.dtype)
        return jnp.einsum('bhqk,bkc->bqhc', p, lat)

    qb_lat = jnp.moveaxis(q_lat.reshape(b, nb, blk, h, c), 1, 0)
    qb_rope = jnp.moveaxis(q_rope.reshape(b, nb, blk, h, MLA_ROPE), 1, 0)
    o = lax.map(block, (jnp.arange(nb), qb_lat, qb_rope))
    return jnp.moveaxis(o, 0, 1).reshape(b, s, h, c)


def mla_attend_sample(q_lat, q_rope, lat, k_rope, past_lat, past_rope):
    sq = q_lat.shape[1]
    n_past = past_lat.shape[1]
    sc_past = mla_scores(q_lat, q_rope, past_lat, past_rope)
    sc_new = mla_scores(q_lat, q_rope, lat, k_rope)
    sc_new = jnp.where(jnp.tril(jnp.ones((sq, sq), bool)), sc_new, -jnp.inf)
    p = jax.nn.softmax(jnp.concatenate([sc_past, sc_new], axis=-1), axis=-1).astype(lat.dtype)
    return (jnp.einsum('bhqk,bkc->bqhc', p[..., :n_past], past_lat)
            + jnp.einsum('bhqk,bkc->bqhc', p[..., n_past:], lat))


def mla_branch(cq, ckv, kr, pos, g_qn, w_uq, g_kvn, w_uk, w_uv, past_lat, past_rope):
    b, s, _ = cq.shape
    cos, sin = rope_cos_sin(pos)
    q = (rmsnorm(cq, g_qn) @ w_uq).reshape(b, s, MLA_HEADS, MLA_NOPE + MLA_ROPE)
    q_nope = q[..., :MLA_NOPE]
    q_rope = apply_rope(q[..., MLA_NOPE:], cos[:, None], sin[:, None])
    lat = rmsnorm(ckv, g_kvn)
    k_rope = apply_rope(kr, cos, sin)
    q_lat = jnp.einsum('bshn,chn->bshc', q_nope, w_uk)
    if past_lat is None:
        o_lat = mla_attend_prompt(q_lat, q_rope, lat, k_rope)
    else:
        o_lat = mla_attend_sample(q_lat, q_rope, lat, k_rope, past_lat, past_rope)
    o = jnp.einsum('bshc,che->bshe', o_lat, w_uv)
    return o.reshape(b, s, MLA_WIDTH), lat, k_rope


def conv_ffn(h, w_up, conv_w, conv_b, w_down, buf):
    s = h.shape[1]
    up = h @ w_up
    ext = jnp.concatenate([buf.astype(up.dtype), up], axis=1)
    conv = conv_b + ext[:, 0:s] * conv_w[0]
    for j in range(1, CONV_W):
        conv = conv + ext[:, j:j + s] * conv_w[j]
    a, g = jnp.split(conv, 2, axis=-1)
    return (a * jax.nn.gelu(g)) @ w_down, ext[:, s:]


def run_layer(x, lp, pos, gla_s0, ssm_x0, conv_buf, past_lat, past_rope):
    b, s, _ = x.shape
    h = rmsnorm(x, lp['g_mix_pre'])
    gate, gq, gk, gv, gog, galr, su, sz, mcq, mckv, mkr = split_cols(h @ lp['w_in'])
    o_a, gla_state = gla_branch(gq, gk, gv, gog, galr, lp['w_gla_a_up'], lp['b_gla_a'], lp['g_gla_norm'], gla_s0)
    o_b, ssm_state = ssm_branch(su, lp['ssm_a_re'], lp['ssm_a_im'], lp['ssm_b_re'], lp['ssm_b_im'],
                                lp['ssm_c_re'], lp['ssm_c_im'], lp['ssm_d'], lp['ssm_log_dt'],
                                lp['w_ssm_glu'], lp['b_ssm_glu'], ssm_x0)
    o_c, sgu_v = sgu_branch(sz, lp['sgu_ln_g'], lp['sgu_ln_b'], lp['w_sgu_s'], lp['b_sgu_s'])
    o_d, lat, k_rope = mla_branch(mcq, mckv, mkr, pos, lp['g_mla_qn'], lp['w_mla_uq'], lp['g_mla_kvn'],
                                  lp['w_mla_uk'], lp['w_mla_uv'], past_lat, past_rope)
    br = jnp.stack([o_a, o_b, o_c, o_d.astype(x.dtype)], axis=2)
    gates = jax.nn.sigmoid(gate.reshape(b, s, N_BRANCH, D_MODEL))
    mix = jnp.sum(gates * jnp.einsum('bsiw,iwd->bsid', br, lp['w_branch']), axis=2) @ lp['w_out']
    x = x + rmsnorm(mix, lp['g_mix_post'])
    f, conv_state = conv_ffn(rmsnorm(x, lp['g_ffn_pre']), lp['w_ffn_up'], lp['ffn_conv_w'], lp['ffn_conv_b'],
                             lp['w_ffn_down'], conv_buf)
    x = x + rmsnorm(f, lp['g_ffn_post'])
    return x, (lat, k_rope, gla_state, ssm_state, conv_state, sgu_v)


def setup_inputs(seed: int = 0) -> dict:
    key = jax.random.key(seed)
    keys = jax.random.split(key, 48)
    cnt = [0]
    f32 = jnp.float32

    def nk():
        k = keys[cnt[0]]
        cnt[0] += 1
        return k

    def nrm(shape, scale=1.0):
        return jax.random.normal(nk(), shape, f32) * scale

    def gain(shape):
        return 1.0 + 0.02 * nrm(shape)

    L, G, N, CG = DEPTH, SSM_GROUPS, SSM_STATE, SSM_GROUP_CH
    n_pages = PAST_LEN // PAGE_SIZE
    n_used = DEC_BATCH * n_pages
    n_pool = n_used + n_used // 4
    n_idx = jnp.arange(N, dtype=f32)
    x_prompt = nrm((BATCH, SEQ, D_MODEL))
    x_sample = nrm((DEC_BATCH, DEC_SEQ, D_MODEL))
    cache_mla_latent = nrm((L, n_pool, PAGE_SIZE, MLA_KV_LORA))
    cache_mla_rope = nrm((L, n_pool, PAGE_SIZE, MLA_ROPE))
    page_table = jax.random.permutation(nk(), n_pool)[:n_used].reshape(DEC_BATCH, n_pages).astype(jnp.int32)
    return {
        'x_prompt': x_prompt,
        'x_sample': x_sample,
        'cache_mla_latent': cache_mla_latent,
        'cache_mla_rope': cache_mla_rope,
        'page_table': page_table,
        'state_gla': nrm((L, DEC_BATCH, GLA_HEADS, GLA_DK, GLA_DV)),
        'state_ssm': nrm((L, DEC_BATCH, G, N, 2), 0.1),
        'state_ffn_conv': nrm((L, DEC_BATCH, CONV_W - 1, 2 * D_FF)),
        'g_mix_pre': gain((L, D_MODEL)),
        'g_mix_post': gain((L, D_MODEL)),
        'g_ffn_pre': gain((L, D_MODEL)),
        'g_ffn_post': gain((L, D_MODEL)),
        'w_in': nrm((L, D_MODEL, N_IN), D_MODEL ** -0.5),
        'w_gla_a_up': nrm((L, GLA_LOWRANK, GLA_HEADS * GLA_DK), GLA_LOWRANK ** -0.5),
        'b_gla_a': nrm((L, GLA_HEADS * GLA_DK), 0.1),
        'g_gla_norm': gain((L, GLA_HEADS, GLA_DV)),
        'ssm_a_re': -0.5 + nrm((L, G, N), 0.01),
        'ssm_a_im': math.pi * n_idx + nrm((L, G, N), 0.01),
        'ssm_b_re': nrm((L, G, N, CG), (2 * CG) ** -0.5),
        'ssm_b_im': nrm((L, G, N, CG), (2 * CG) ** -0.5),
        'ssm_c_re': nrm((L, G, CG, N), N ** -0.5),
        'ssm_c_im': nrm((L, G, CG, N), N ** -0.5),
        'ssm_d': nrm((L, SSM_WIDTH)),
        'ssm_log_dt': jax.random.uniform(nk(), (L, G), f32, math.log(1e-3), math.log(1e-1)),
        'w_ssm_glu': nrm((L, SSM_WIDTH, SSM_WIDTH), SSM_WIDTH ** -0.5),
        'b_ssm_glu': nrm((L, SSM_WIDTH), 0.02),
        'sgu_ln_g': gain((L, SGU_WIDTH)),
        'sgu_ln_b': nrm((L, SGU_WIDTH), 0.02),
        'w_sgu_s': nrm((L, SGU_HEADS, SGU_CHUNK, SGU_CHUNK), SGU_CHUNK ** -0.5),
        'b_sgu_s': 1.0 + nrm((L, SGU_HEADS, SGU_CHUNK), 0.02),
        'g_mla_qn': gain((L, MLA_Q_LORA)),
        'w_mla_uq': nrm((L, MLA_Q_LORA, MLA_HEADS * (MLA_NOPE + MLA_ROPE)), MLA_Q_LORA ** -0.5),
        'g_mla_kvn': gain((L, MLA_KV_LORA)),
        'w_mla_uk': nrm((L, MLA_KV_LORA, MLA_HEADS, MLA_NOPE), MLA_KV_LORA ** -0.5),
        'w_mla_uv': nrm((L, MLA_KV_LORA, MLA_HEADS, MLA_V), MLA_KV_LORA ** -0.5),
        'w_branch': nrm((L, N_BRANCH, BRANCH_W, D_MODEL), BRANCH_W ** -0.5),
        'w_out': nrm((L, D_MODEL, D_MODEL), D_MODEL ** -0.5),
        'w_ffn_up': nrm((L, D_MODEL, 2 * D_FF), D_MODEL ** -0.5),
        'ffn_conv_w': nrm((L, CONV_W, 2 * D_FF), CONV_W ** -0.5),
        'ffn_conv_b': nrm((L, 2 * D_FF), 0.02),
        'w_ffn_down': nrm((L, D_FF, D_MODEL), D_FF ** -0.5),
    }


def reference(x_prompt, x_sample, cache_mla_latent, cache_mla_rope, page_table, state_gla, state_ssm,
              state_ffn_conv, g_mix_pre, g_mix_post, g_ffn_pre, g_ffn_post, w_in, w_gla_a_up, b_gla_a,
              g_gla_norm, ssm_a_re, ssm_a_im, ssm_b_re, ssm_b_im, ssm_c_re, ssm_c_im, ssm_d, ssm_log_dt,
              w_ssm_glu, b_ssm_glu, sgu_ln_g, sgu_ln_b, w_sgu_s, b_sgu_s, g_mla_qn, w_mla_uq, g_mla_kvn,
              w_mla_uk, w_mla_uv, w_branch, w_out, w_ffn_up, ffn_conv_w, ffn_conv_b, w_ffn_down):
    b = x_prompt.shape[0]
    n_dec = x_sample.shape[0]
    past_len = page_table.shape[1] * PAGE_SIZE
    pos_p = jnp.arange(x_prompt.shape[1], dtype=jnp.int32)
    pos_s = past_len + jnp.arange(x_sample.shape[1], dtype=jnp.int32)
    gla0 = jnp.zeros((b, GLA_HEADS, GLA_DK, GLA_DV), jnp.float32)
    ssm0 = jnp.zeros((b, SSM_GROUPS, SSM_STATE, 2), jnp.float32)
    conv0 = jnp.zeros((b, CONV_W - 1, 2 * D_FF), x_prompt.dtype)
    yp, ys = x_prompt, x_sample
    st_p, st_s = [], []
    for l in range(DEPTH):
        lp = dict(g_mix_pre=g_mix_pre[l], g_mix_post=g_mix_post[l], g_ffn_pre=g_ffn_pre[l],
                  g_ffn_post=g_ffn_post[l], w_in=w_in[l], w_gla_a_up=w_gla_a_up[l], b_gla_a=b_gla_a[l],
                  g_gla_norm=g_gla_norm[l], ssm_a_re=ssm_a_re[l], ssm_a_im=ssm_a_im[l], ssm_b_re=ssm_b_re[l],
                  ssm_b_im=ssm_b_im[l], ssm_c_re=ssm_c_re[l], ssm_c_im=ssm_c_im[l], ssm_d=ssm_d[l],
                  ssm_log_dt=ssm_log_dt[l], w_ssm_glu=w_ssm_glu[l], b_ssm_glu=b_ssm_glu[l],
                  sgu_ln_g=sgu_ln_g[l], sgu_ln_b=sgu_ln_b[l], w_sgu_s=w_sgu_s[l], b_sgu_s=b_sgu_s[l],
                  g_mla_qn=g_mla_qn[l], w_mla_uq=w_mla_uq[l], g_mla_kvn=g_mla_kvn[l], w_mla_uk=w_mla_uk[l],
                  w_mla_uv=w_mla_uv[l], w_branch=w_branch[l], w_out=w_out[l], w_ffn_up=w_ffn_up[l],
                  ffn_conv_w=ffn_conv_w[l], ffn_conv_b=ffn_conv_b[l], w_ffn_down=w_ffn_down[l])
        yp, sp = run_layer(yp, lp, pos_p, gla0, ssm0, conv0, None, None)
        past_lat = cache_mla_latent[l, page_table].reshape(n_dec, -1, MLA_KV_LORA)
        past_rope = cache_mla_rope[l, page_table].reshape(n_dec, -1, MLA_ROPE)
        ys, ss = run_layer(ys, lp, pos_s, state_gla[l], state_ssm[l], state_ffn_conv[l], past_lat, past_rope)
        st_p.append(sp)
        st_s.append(ss)
    lat_p, rope_p, gla_p, ssm_p, conv_p, sgu_p = [jnp.stack(t) for t in zip(*st_p)]
    lat_s, rope_s, gla_s, ssm_s, conv_s, sgu_s = [jnp.stack(t) for t in zip(*st_s)]
    return (yp, ys, lat_p, lat_s, rope_p, rope_s, gla_p, gla_s, ssm_p, ssm_s, conv_p, conv_s, sgu_p, sgu_s)
```

```python
import functools
import math

import jax
import jax.numpy as jnp
from jax import lax
from jax.experimental import pallas as pl
from jax.experimental.pallas import tpu as pltpu

F32 = jnp.float32
BF16 = jnp.bfloat16

D_MODEL = 2048
EPS = 1e-6
N_BRANCH = 4
GLA_HEADS, GLA_DK, GLA_DV = 4, 64, 128
GLA_LOWRANK, GLA_TAU, GLA_CHUNK = 16, 16.0, 64
SSM_WIDTH, SSM_GROUP_CH, SSM_STATE = 512, 16, 64
SSM_GROUPS = SSM_WIDTH // SSM_GROUP_CH
SSM_LANES = SSM_GROUPS * SSM_STATE
SGU_WIDTH, SGU_HEADS, SGU_CHUNK = 512, 4, 128
SGU_HEAD_W = SGU_WIDTH // SGU_HEADS
MLA_HEADS, MLA_Q_LORA, MLA_KV_LORA, MLA_NOPE, MLA_ROPE, MLA_V = 4, 384, 128, 128, 64, 128
MLA_SCALE = (MLA_NOPE + MLA_ROPE) ** -0.5
ROPE_THETA = 10000.0
PAGE_SIZE = 128
BRANCH_W = 512
D_FF = 5632
CONV_W = 3

LANES = 128
SUBLANES = 8
VMEM_LIMIT_BYTES = 56 * 1024 * 1024

COL_GATE = 0
COL_GQ = 8192
COL_GK = 8448
COL_GV = 8704
COL_GOG = 9216
COL_SU = 9728
COL_SZ = 10240
COL_CKV = 11264
COL_KR = 11392
COL_CQ = 11520
COL_KRS = 11904
COL_ALR = 12032
N_IN_PAD = 12288


def _params(*sem):
    return pltpu.CompilerParams(dimension_semantics=sem, vmem_limit_bytes=VMEM_LIMIT_BYTES)


def _dot(a, b):
    return jnp.dot(a, b, preferred_element_type=F32)


def _dot_nt(a, b):
    return lax.dot_general(a, b, (((1,), (1,)), ((), ())), preferred_element_type=F32)


def _dot_tn(a, b):
    return lax.dot_general(a, b, (((0,), (0,)), ((), ())), preferred_element_type=F32)


def _rms(x, g):
    return x * lax.rsqrt(jnp.mean(x * x, axis=-1, keepdims=True) + EPS) * g


def _gelu(x):
    return 0.5 * x * (1.0 + jnp.tanh(math.sqrt(2.0 / math.pi) * (x + 0.044715 * (x * x * x))))


def _sigmoid(x):
    return 1.0 / (1.0 + jnp.exp(-x))


def _log_sigmoid(x):
    return jnp.minimum(x, 0.0) - jnp.log(1.0 + jnp.exp(-jnp.abs(x)))


def _norm_matmul_kernel(x_ref, g_ref, w_ref, o_ref, xn_ref):
    @pl.when(pl.program_id(1) == 0)
    def _():
        xn_ref[...] = _rms(x_ref[...], g_ref[...]).astype(BF16)

    o_ref[...] = _dot(xn_ref[...], w_ref[...])


def norm_matmul(x, g, w, tm, tn):
    m, k = x.shape
    n = w.shape[1]
    return pl.pallas_call(
        _norm_matmul_kernel,
        grid=(m // tm, n // tn),
        in_specs=[pl.BlockSpec((tm, k), lambda i, j: (i, 0)),
                  pl.BlockSpec((1, k), lambda i, j: (0, 0)),
                  pl.BlockSpec((k, tn), lambda i, j: (0, j))],
        out_specs=pl.BlockSpec((tm, tn), lambda i, j: (i, j)),
        out_shape=jax.ShapeDtypeStruct((m, n), F32),
        scratch_shapes=[pltpu.VMEM((tm, k), BF16)],
        compiler_params=_params("parallel", "arbitrary"),
        name="norm_matmul",
    )(x, g, w)


def _gla_prompt_kernel(q_ref, k_ref, v_ref, og_ref, alr_ref, wup_ref, ba_ref, gn_ref,
                       o_ref, st_ref, state_ref, obuf_ref, *, blocks_per_seq, n_chunks):
    i = pl.program_id(0)
    c = GLA_CHUNK
    hk = GLA_HEADS * GLA_DK
    hv = GLA_HEADS * GLA_DV

    @pl.when(i % blocks_per_seq == 0)
    def _():
        state_ref[...] = jnp.zeros_like(state_ref)

    alr = alr_ref[:, 0:GLA_LOWRANK]
    la = _log_sigmoid(_dot(alr.astype(BF16), wup_ref[...]) + ba_ref[...]) * (1.0 / GLA_TAU)

    r = lax.broadcasted_iota(jnp.int32, (c, c), 0)
    cc = lax.broadcasted_iota(jnp.int32, (c, c), 1)
    tril = (cc <= r)
    tril_f = tril.astype(F32)
    lane_k = lax.broadcasted_iota(jnp.int32, (1, hk), 1) >> 6
    sr = lax.broadcasted_iota(jnp.int32, (hv, hk), 0) >> 7
    sc = lax.broadcasted_iota(jnp.int32, (hv, hk), 1) >> 6
    diag = sr == sc
    causal4 = jnp.concatenate([tril] * GLA_HEADS, axis=0)

    for ci in range(n_chunks):
        rows = pl.ds(ci * c, c)
        la_c = la[ci * c:(ci + 1) * c]
        cum = jnp.dot(tril_f, la_c, preferred_element_type=F32, precision=lax.Precision.HIGHEST)
        last = cum[c - 1:c, :]
        qc = q_ref[rows, :] * (GLA_DK ** -0.5)
        kc = k_ref[rows, :]
        vc = v_ref[rows, :].astype(BF16)
        q_dec = qc * jnp.exp(cum)
        k_dec = (kc * jnp.exp(-cum)).astype(BF16)
        k_tail = (kc * jnp.exp(last - cum)).astype(BF16)
        q_stack = jnp.concatenate(
            [jnp.where(lane_k == h, q_dec, 0.0) for h in range(GLA_HEADS)], axis=0).astype(BF16)
        att = _dot_nt(q_stack, k_dec)
        att = jnp.where(causal4, att, 0.0).astype(BF16)
        o_full = _dot(att, vc)
        o_intra = jnp.concatenate(
            [o_full[h * c:(h + 1) * c, h * GLA_DV:(h + 1) * GLA_DV] for h in range(GLA_HEADS)], axis=1)
        st = state_ref[...]
        o_inter = _dot_nt(q_dec.astype(BF16), st.astype(BF16))
        obuf_ref[rows, :] = o_intra + o_inter
        upd = _dot_tn(vc, k_tail)
        state_ref[...] = st * jnp.exp(last) + jnp.where(diag, upd, 0.0)

    o = obuf_ref[...]
    og = og_ref[...]
    gn = gn_ref[...]
    outs = []
    for h in range(GLA_HEADS):
        sl = slice(h * GLA_DV, (h + 1) * GLA_DV)
        outs.append(_rms(o[:, sl], gn[:, sl]))
    o_n = jnp.concatenate(outs, axis=1)
    o_ref[...] = (o_n * (og * _sigmoid(og))).astype(BF16)
    st_ref[0] = state_ref[...]


def gla_prompt(p, wup, ba, gn, n_seq, seq, t_blk):
    m = p.shape[0]
    bps = seq // t_blk
    kern = functools.partial(_gla_prompt_kernel, blocks_per_seq=bps, n_chunks=t_blk // GLA_CHUNK)
    return pl.pallas_call(
        kern,
        grid=(m // t_blk,),
        in_specs=[pl.BlockSpec((t_blk, 256), lambda i: (i, COL_GQ // 256)),
                  pl.BlockSpec((t_blk, 256), lambda i: (i, COL_GK // 256)),
                  pl.BlockSpec((t_blk, 512), lambda i: (i, COL_GV // 512)),
                  pl.BlockSpec((t_blk, 512), lambda i: (i, COL_GOG // 512)),
                  pl.BlockSpec((t_blk, LANES), lambda i: (i, COL_ALR // LANES)),
                  pl.BlockSpec((GLA_LOWRANK, 256), lambda i: (0, 0)),
                  pl.BlockSpec((1, 256), lambda i: (0, 0)),
                  pl.BlockSpec((1, 512), lambda i: (0, 0))],
        out_specs=[pl.BlockSpec((t_blk, 512), lambda i: (i, 0)),
                   pl.BlockSpec((1, 512, 256), lambda i: (i // bps, 0, 0))],
        out_shape=[jax.ShapeDtypeStruct((m, 512), BF16),
                   jax.ShapeDtypeStruct((n_seq, 512, 256), F32)],
        scratch_shapes=[pltpu.VMEM((512, 256), F32), pltpu.VMEM((t_blk, 512), F32)],
        compiler_params=_params("arbitrary"),
        name="gla_prompt",
    )(p, p, p, p, p, wup, ba, gn)


def _gla_decode_kernel(q_ref, k_ref, v_ref, og_ref, alr_ref, wup_ref, ba_ref, gn_ref, s0_ref,
                       o_ref, s1_ref):
    alr = alr_ref[:, 0:GLA_LOWRANK]
    la = _log_sigmoid(_dot(alr.astype(BF16), wup_ref[0]) + ba_ref[0]) * (1.0 / GLA_TAU)
    a = jnp.exp(la)
    q = q_ref[0] * (GLA_DK ** -0.5)
    k = k_ref[0]
    v = v_ref[...]
    q_dec = q * a
    att = jnp.sum(q_dec * (k * jnp.exp(-la)), axis=1, keepdims=True)
    o = att * v
    for dk in range(GLA_DK):
        s0 = s0_ref[:, 0, dk, :]
        o = o + q_dec[:, dk:dk + 1] * s0
        s1_ref[:, 0, dk, :] = s0 * a[:, dk:dk + 1] + k[:, dk:dk + 1] * v
    og = og_ref[...]
    o_ref[...] = (_rms(o, gn_ref[...]) * (og * _sigmoid(og))).astype(BF16)


def gla_decode(p, q_h, k_h, wup_h, ba_h, gn, s0):
    b = p.shape[0]
    return pl.pallas_call(
        _gla_decode_kernel,
        grid=(GLA_HEADS,),
        in_specs=[pl.BlockSpec((1, b, GLA_DK), lambda h: (h, 0, 0)),
                  pl.BlockSpec((1, b, GLA_DK), lambda h: (h, 0, 0)),
                  pl.BlockSpec((b, GLA_DV), lambda h: (0, COL_GV // GLA_DV + h)),
                  pl.BlockSpec((b, GLA_DV), lambda h: (0, COL_GOG // GLA_DV + h)),
                  pl.BlockSpec((b, LANES), lambda h: (0, COL_ALR // LANES)),
                  pl.BlockSpec((1, GLA_LOWRANK, GLA_DK), lambda h: (h, 0, 0)),
                  pl.BlockSpec((1, 1, GLA_DK), lambda h: (h, 0, 0)),
                  pl.BlockSpec((1, GLA_DV), lambda h: (0, h)),
                  pl.BlockSpec((b, 1, GLA_DK, GLA_DV), lambda h: (0, h, 0, 0))],
        out_specs=[pl.BlockSpec((b, GLA_DV), lambda h: (0, h)),
                   pl.BlockSpec((b, 1, GLA_DK, GLA_DV), lambda h: (0, h, 0, 0))],
        out_shape=[jax.ShapeDtypeStruct((b, GLA_HEADS * GLA_DV), BF16),
                   jax.ShapeDtypeStruct(s0.shape, F32)],
        compiler_params=_params("arbitrary"),
        name="gla_decode",
    )(q_h, k_h, p, p, p, wup_h, ba_h, gn, s0)


SSM_SB = 4
SSM_SB_LANES = SSM_LANES // SSM_SB


def _ssm_in(u, bre_ref, bim_ref, xr_ref, xi_ref):
    for sb in range(SSM_SB):
        ub = u[:, sb * LANES:(sb + 1) * LANES].astype(BF16)
        cols = slice(sb * SSM_SB_LANES, (sb + 1) * SSM_SB_LANES)
        xr_ref[:, cols] = _dot(ub, bre_ref[sb])
        xi_ref[:, cols] = _dot(ub, bim_ref[sb])


def _ssm_out(u, xr_ref, xi_ref, cre_ref, cim_ref, d_ref, wglu_ref, bglu_ref):
    ys = []
    for sb in range(SSM_SB):
        cols = slice(sb * SSM_SB_LANES, (sb + 1) * SSM_SB_LANES)
        ys.append(_dot(xr_ref[:, cols].astype(BF16), cre_ref[sb])
                  - _dot(xi_ref[:, cols].astype(BF16), cim_ref[sb]))
    return _ssm_glu(jnp.concatenate(ys, axis=1), u, d_ref, wglu_ref, bglu_ref)


def _ssm_glu(y, u, d_ref, wglu_ref, bglu_ref):
    y = _gelu(y + d_ref[...] * u)
    return (y * _sigmoid(_dot(y.astype(BF16), wglu_ref[...]) + bglu_ref[...])).astype(BF16)


def _ssm_prompt_kernel(u_ref, bre_ref, bim_ref, ar_ref, ai_ref, pwr_ref, pwi_ref, cre_ref, cim_ref,
                       d_ref, wglu_ref, bglu_ref, o_ref, st_ref,
                       xr_ref, xi_ref, cr_ref, ci_ref, inr_ref, ini_ref, *, blocks_per_seq, seg_len):
    i = pl.program_id(0)
    nseg = SUBLANES

    @pl.when(i % blocks_per_seq == 0)
    def _():
        cr_ref[...] = jnp.zeros_like(cr_ref)
        ci_ref[...] = jnp.zeros_like(ci_ref)

    u = u_ref[...]
    spb = SSM_SB_LANES // LANES
    for sb in range(SSM_SB):
        ub = u[:, sb * LANES:(sb + 1) * LANES].astype(BF16)
        xr = _dot(ub, bre_ref[sb])
        xi = _dot(ub, bim_ref[sb])
        for s in range(spb):
            xr_ref[sb * spb + s] = xr[:, s * LANES:(s + 1) * LANES]
            xi_ref[sb * spb + s] = xi[:, s * LANES:(s + 1) * LANES]

    zero = jnp.zeros((nseg, LANES), F32)
    for sb in range(SSM_SB):
        slabs = [sb * spb + s for s in range(spb)]
        lane = [pl.ds(s * LANES, LANES) for s in slabs]
        ar = [jnp.broadcast_to(ar_ref[:, c], (nseg, LANES)) for c in lane]
        ai = [jnp.broadcast_to(ai_ref[:, c], (nseg, LANES)) for c in lane]

        def local(t, carry):
            rows = pl.ds(t, nseg, stride=seg_len)
            out = []
            for n, s in enumerate(slabs):
                sr, si = carry[2 * n], carry[2 * n + 1]
                nr = ar[n] * sr - ai[n] * si + xr_ref[s, rows, :]
                ni = ar[n] * si + ai[n] * sr + xi_ref[s, rows, :]
                xr_ref[s, rows, :] = nr
                xi_ref[s, rows, :] = ni
                out += [nr, ni]
            return tuple(out)

        ends = lax.fori_loop(0, seg_len, local, (zero,) * (2 * spb))

        ins = []
        for n, c in enumerate(lane):
            er, ei = ends[2 * n], ends[2 * n + 1]
            alr = pwr_ref[seg_len - 1, 0:1, c]
            ali = pwi_ref[seg_len - 1, 0:1, c]
            pr = cr_ref[:, c]
            pi_ = ci_ref[:, c]
            loc = pl.ds(n * LANES, LANES)
            for j in range(nseg):
                inr_ref[j:j + 1, loc] = pr
                ini_ref[j:j + 1, loc] = pi_
                nr = alr * pr - ali * pi_ + er[j:j + 1, :]
                ni = alr * pi_ + ali * pr + ei[j:j + 1, :]
                pr, pi_ = nr, ni
            cr_ref[:, c] = pr
            ci_ref[:, c] = pi_
            ins.append((inr_ref[:, loc], ini_ref[:, loc]))

        def fix(t, _):
            rows = pl.ds(t, nseg, stride=seg_len)
            for n, s in enumerate(slabs):
                inr, ini = ins[n]
                pwr = pwr_ref[t, :, lane[n]]
                pwi = pwi_ref[t, :, lane[n]]
                xr_ref[s, rows, :] = xr_ref[s, rows, :] + (pwr * inr - pwi * ini)
                xi_ref[s, rows, :] = xi_ref[s, rows, :] + (pwr * ini + pwi * inr)
            return 0

        lax.fori_loop(0, seg_len, fix, 0)

    ys = []
    for sb in range(SSM_SB):
        hr = jnp.concatenate([xr_ref[sb * spb + s] for s in range(spb)], axis=1).astype(BF16)
        hi = jnp.concatenate([xi_ref[sb * spb + s] for s in range(spb)], axis=1).astype(BF16)
        ys.append(_dot(hr, cre_ref[sb]) - _dot(hi, cim_ref[sb]))
    o_ref[...] = _ssm_glu(jnp.concatenate(ys, axis=1), u, d_ref, wglu_ref, bglu_ref)
    st_ref[0, 0:1, :] = cr_ref[...]
    st_ref[0, 1:2, :] = ci_ref[...]


def ssm_prompt(p, sw, n_seq, seq, t_blk):
    m = p.shape[0]
    bps = seq // t_blk
    seg_len = t_blk // SUBLANES
    kern = functools.partial(_ssm_prompt_kernel, blocks_per_seq=bps, seg_len=seg_len)
    full = lambda *s: pl.BlockSpec(s, lambda i: (0,) * len(s))
    return pl.pallas_call(
        kern,
        grid=(m // t_blk,),
        in_specs=[pl.BlockSpec((t_blk, 512), lambda i: (i, COL_SU // 512)),
                  full(SSM_SB, LANES, SSM_SB_LANES), full(SSM_SB, LANES, SSM_SB_LANES),
                  full(1, SSM_LANES), full(1, SSM_LANES),
                  full(seg_len, SUBLANES, SSM_LANES), full(seg_len, SUBLANES, SSM_LANES),
                  full(SSM_SB, SSM_SB_LANES, LANES), full(SSM_SB, SSM_SB_LANES, LANES),
                  full(1, 512), full(512, 512), full(1, 512)],
        out_specs=[pl.BlockSpec((t_blk, 512), lambda i: (i, 0)),
                   pl.BlockSpec((1, 2, SSM_LANES), lambda i: (i // bps, 0, 0))],
        out_shape=[jax.ShapeDtypeStruct((m, 512), BF16),
                   jax.ShapeDtypeStruct((n_seq, 2, SSM_LANES), F32)],
        scratch_shapes=[pltpu.VMEM((SSM_LANES // LANES, t_blk, LANES), F32),
                        pltpu.VMEM((SSM_LANES // LANES, t_blk, LANES), F32),
                        pltpu.VMEM((1, SSM_LANES), F32), pltpu.VMEM((1, SSM_LANES), F32),
                        pltpu.VMEM((SUBLANES, SSM_SB_LANES), F32), pltpu.VMEM((SUBLANES, SSM_SB_LANES), F32)],
        compiler_params=_params("arbitrary"),
        name="ssm_prompt",
    )(p, sw["bre"], sw["bim"], sw["ar"], sw["ai"], sw["pwr"], sw["pwi"],
      sw["cre"], sw["cim"], sw["d"], sw["wglu"], sw["bglu"])


def _ssm_decode_kernel(u_ref, x0r_ref, x0i_ref, bre_ref, bim_ref, ar_ref, ai_ref, cre_ref, cim_ref,
                       d_ref, wglu_ref, bglu_ref, o_ref, hr_ref, hi_ref):
    u = u_ref[...]
    _ssm_in(u, bre_ref, bim_ref, hr_ref, hi_ref)
    ar = ar_ref[...]
    ai = ai_ref[...]
    x0r = x0r_ref[...]
    x0i = x0i_ref[...]
    hr_ref[...] = hr_ref[...] + (ar * x0r - ai * x0i)
    hi_ref[...] = hi_ref[...] + (ar * x0i + ai * x0r)
    o_ref[...] = _ssm_out(u, hr_ref, hi_ref, cre_ref, cim_ref, d_ref, wglu_ref, bglu_ref)


def ssm_decode(p, x0r, x0i, sw):
    b = p.shape[0]
    full = lambda *s: pl.BlockSpec(s, lambda i: (0,) * len(s))
    return pl.pallas_call(
        _ssm_decode_kernel,
        grid=(1,),
        in_specs=[pl.BlockSpec((b, 512), lambda i: (0, COL_SU // 512)),
                  full(b, SSM_LANES), full(b, SSM_LANES),
                  full(SSM_SB, LANES, SSM_SB_LANES), full(SSM_SB, LANES, SSM_SB_LANES),
                  full(1, SSM_LANES), full(1, SSM_LANES),
                  full(SSM_SB, SSM_SB_LANES, LANES), full(SSM_SB, SSM_SB_LANES, LANES),
                  full(1, 512), full(512, 512), full(1, 512)],
        out_specs=[full(b, 512), full(b, SSM_LANES), full(b, SSM_LANES)],
        out_shape=[jax.ShapeDtypeStruct((b, 512), BF16),
                   jax.ShapeDtypeStruct((b, SSM_LANES), F32),
                   jax.ShapeDtypeStruct((b, SSM_LANES), F32)],
        compiler_params=_params("arbitrary"),
        name="ssm_decode",
    )(p, x0r, x0i, sw["bre"], sw["bim"], sw["ar"], sw["ai"], sw["cre"], sw["cim"],
      sw["d"], sw["wglu"], sw["bglu"])


def _sgu_uv(z, lng, lnb):
    gz = _gelu(z)
    u = gz[:, :SGU_WIDTH]
    v = gz[:, SGU_WIDTH:]
    vc = v - jnp.mean(v, axis=-1, keepdims=True)
    v = vc * lax.rsqrt(jnp.mean(vc * vc, axis=-1, keepdims=True) + EPS) * lng + lnb
    return u, v


def _sgu_prompt_kernel(z_ref, lng_ref, lnb_ref, ws_ref, bs_ref, o_ref, vst_ref, *, n_chunks):
    u, v = _sgu_uv(z_ref[...], lng_ref[...], lnb_ref[...])
    c = SGU_CHUNK
    r = lax.broadcasted_iota(jnp.int32, (c, c), 0)
    cc = lax.broadcasted_iota(jnp.int32, (c, c), 1)
    for h in range(SGU_HEADS):
        w = jnp.where(cc <= r, ws_ref[h], 0.0).astype(BF16)
        bias = bs_ref[h]
        lanes = slice(h * SGU_HEAD_W, (h + 1) * SGU_HEAD_W)
        for ci in range(n_chunks):
            rows = slice(ci * c, (ci + 1) * c)
            mixed = _dot(w, v[rows, lanes].astype(BF16)) + bias
            o_ref[rows, lanes] = (u[rows, lanes] * mixed).astype(BF16)
    vst_ref[0] = v[(n_chunks - 1) * c:, :]


def sgu_prompt(p, lng, lnb, ws, bsb, n_seq, seq, t_blk):
    m = p.shape[0]
    bps = seq // t_blk
    kern = functools.partial(_sgu_prompt_kernel, n_chunks=t_blk // SGU_CHUNK)
    return pl.pallas_call(
        kern,
        grid=(m // t_blk,),
        in_specs=[pl.BlockSpec((t_blk, 1024), lambda i: (i, COL_SZ // 1024)),
                  pl.BlockSpec((1, 512), lambda i: (0, 0)),
                  pl.BlockSpec((1, 512), lambda i: (0, 0)),
                  pl.BlockSpec((SGU_HEADS, SGU_CHUNK, SGU_CHUNK), lambda i: (0, 0, 0)),
                  pl.BlockSpec((SGU_HEADS, SGU_CHUNK, SGU_HEAD_W), lambda i: (0, 0, 0))],
        out_specs=[pl.BlockSpec((t_blk, 512), lambda i: (i, 0)),
                   pl.BlockSpec((1, SGU_CHUNK, 512), lambda i: (i // bps, 0, 0))],
        out_shape=[jax.ShapeDtypeStruct((m, 512), BF16),
                   jax.ShapeDtypeStruct((n_seq, SGU_CHUNK, 512), F32)],
        compiler_params=_params("arbitrary"),
        name="sgu_prompt",
    )(p, lng, lnb, ws, bsb)


def _sgu_decode_kernel(z_ref, lng_ref, lnb_ref, w0_ref, b0_ref, o_ref, v_ref):
    u, v = _sgu_uv(z_ref[...], lng_ref[...], lnb_ref[...])
    o_ref[...] = (u * (w0_ref[...] * v + b0_ref[...])).astype(BF16)
    v_ref[...] = v


def sgu_decode(p, lng, lnb, w0, b0):
    b = p.shape[0]
    full = lambda *s: pl.BlockSpec(s, lambda i: (0,) * len(s))
    return pl.pallas_call(
        _sgu_decode_kernel,
        grid=(1,),
        in_specs=[pl.BlockSpec((b, 1024), lambda i: (0, COL_SZ // 1024)),
                  full(1, 512), full(1, 512), full(1, 512), full(1, 512)],
        out_specs=[full(b, 512), full(b, 512)],
        out_shape=[jax.ShapeDtypeStruct((b, 512), BF16), jax.ShapeDtypeStruct((b, 512), F32)],
        compiler_params=_params("arbitrary"),
        name="sgu_decode",
    )(p, lng, lnb, w0, b0)


QK_W = 256


def _mla_prep_kernel(cq_ref, ckv_ref, kr_ref, krs_ref, cos_ref, sin_ref, gq_ref, wuq_ref, wuk_ref, gkv_ref,
                     q_ref, kcat_ref, lat_ref, rope_ref):
    cos = cos_ref[...]
    sin = sin_ref[...]
    cqn = _rms(cq_ref[...], gq_ref[...]).astype(BF16)
    qall = _dot(cqn, wuq_ref[...])
    for h in range(MLA_HEADS):
        base = h * QK_W
        q_nope = qall[:, base:base + MLA_NOPE].astype(BF16)
        q_lat = _dot(q_nope, wuk_ref[h])
        q_rot = (qall[:, base + MLA_NOPE:base + QK_W] * cos
                 + qall[:, MLA_HEADS * QK_W + h * LANES:MLA_HEADS * QK_W + (h + 1) * LANES] * sin)
        q_ref[h] = (jnp.concatenate([q_lat, q_rot], axis=1) * MLA_SCALE).astype(BF16)
    lat = _rms(ckv_ref[...], gkv_ref[...])
    k_rot = kr_ref[...] * cos + krs_ref[...] * sin
    lat_ref[...] = lat
    rope_ref[...] = k_rot[:, 0:MLA_ROPE]
    kcat_ref[...] = jnp.concatenate([lat, k_rot], axis=1).astype(BF16)


def mla_prep(p, cos, sin, gq, wuq, wuk, gkv, tm, tiles_per_seq):
    m = p.shape[0]
    full = lambda *s: pl.BlockSpec(s, lambda i: (0,) * len(s))
    return pl.pallas_call(
        _mla_prep_kernel,
        grid=(m // tm,),
        in_specs=[pl.BlockSpec((tm, MLA_Q_LORA), lambda i: (i, COL_CQ // MLA_Q_LORA)),
                  pl.BlockSpec((tm, LANES), lambda i: (i, COL_CKV // LANES)),
                  pl.BlockSpec((tm, LANES), lambda i: (i, COL_KR // LANES)),
                  pl.BlockSpec((tm, LANES), lambda i: (i, COL_KRS // LANES)),
                  pl.BlockSpec((tm, LANES), lambda i: (i % tiles_per_seq, 0)),
                  pl.BlockSpec((tm, LANES), lambda i: (i % tiles_per_seq, 0)),
                  full(1, MLA_Q_LORA), full(MLA_Q_LORA, 6 * QK_W), full(MLA_HEADS, MLA_NOPE, MLA_KV_LORA),
                  full(1, MLA_KV_LORA)],
        out_specs=[pl.BlockSpec((MLA_HEADS, tm, QK_W), lambda i: (0, i, 0)),
                   pl.BlockSpec((tm, QK_W), lambda i: (i, 0)),
                   pl.BlockSpec((tm, MLA_KV_LORA), lambda i: (i, 0)),
                   pl.BlockSpec((tm, MLA_ROPE), lambda i: (i, 0))],
        out_shape=[jax.ShapeDtypeStruct((MLA_HEADS, m, QK_W), BF16),
                   jax.ShapeDtypeStruct((m, QK_W), BF16),
                   jax.ShapeDtypeStruct((m, MLA_KV_LORA), F32),
                   jax.ShapeDtypeStruct((m, MLA_ROPE), F32)],
        compiler_params=_params("parallel"),
        name="mla_prep",
    )(p, p, p, p, cos, sin, gq, wuq, wuk, gkv)


def _flash_kernel(q_ref, k_ref, wuv_ref, o_ref, m_ref, l_ref, acc_ref, *, tq, tk, nk):
    qi = pl.program_id(1)
    ki = pl.program_id(2)
    rows = MLA_HEADS * tq

    @pl.when(ki == 0)
    def _():
        m_ref[...] = jnp.full_like(m_ref, -jnp.inf)
        l_ref[...] = jnp.zeros_like(l_ref)
        acc_ref[...] = jnp.zeros_like(acc_ref)

    @pl.when(ki * tk <= qi * tq + (tq - 1))
    def _():
        q = q_ref[...].reshape(rows, QK_W)
        k = k_ref[...]
        s = _dot_nt(q, k)
        qpos = qi * tq + (lax.broadcasted_iota(jnp.int32, (rows, tk), 0) & (tq - 1))
        kpos = ki * tk + lax.broadcasted_iota(jnp.int32, (rows, tk), 1)
        s = jnp.where(kpos <= qpos, s, -jnp.inf)
        m_old = m_ref[...]
        m_new = jnp.maximum(m_old, jnp.max(s, axis=1, keepdims=True))
        alpha = jnp.exp(m_old - m_new)
        pexp = jnp.exp(s - m_new)
        l_ref[...] = alpha * l_ref[...] + jnp.sum(pexp, axis=1, keepdims=True)
        acc_ref[...] = alpha * acc_ref[...] + _dot(pexp.astype(BF16), k[:, 0:MLA_KV_LORA])
        m_ref[...] = m_new

    @pl.when(ki == nk - 1)
    def _():
        o_lat = (acc_ref[...] / l_ref[...]).astype(BF16)
        for h in range(MLA_HEADS):
            o_ref[:, h * MLA_V:(h + 1) * MLA_V] = _dot(o_lat[h * tq:(h + 1) * tq], wuv_ref[h]).astype(BF16)


def mla_flash(q, kcat, wuv, n_seq, seq, tq, tk):
    m = kcat.shape[0]
    nq, nk = seq // tq, seq // tk
    kern = functools.partial(_flash_kernel, tq=tq, tk=tk, nk=nk)

    def k_map(b, qi, ki):
        return (b * nk + jnp.minimum(ki, (qi * tq + tq - 1) // tk), 0)

    return pl.pallas_call(
        kern,
        grid=(n_seq, nq, nk),
        in_specs=[pl.BlockSpec((MLA_HEADS, tq, QK_W), lambda b, qi, ki: (0, b * nq + qi, 0)),
                  pl.BlockSpec((tk, QK_W), k_map),
                  pl.BlockSpec((MLA_HEADS, MLA_KV_LORA, MLA_V), lambda b, qi, ki: (0, 0, 0))],
        out_specs=pl.BlockSpec((tq, MLA_HEADS * MLA_V), lambda b, qi, ki: (b * nq + qi, 0)),
        out_shape=jax.ShapeDtypeStruct((m, MLA_HEADS * MLA_V), BF16),
        scratch_shapes=[pltpu.VMEM((MLA_HEADS * tq, 1), F32), pltpu.VMEM((MLA_HEADS * tq, 1), F32),
                        pltpu.VMEM((MLA_HEADS * tq, MLA_KV_LORA), F32)],
        compiler_params=_params("parallel", "parallel", "arbitrary"),
        name="mla_flash",
    )(q, kcat, wuv)


Q_ROWS = 8


def _paged_kernel(pt_ref, q_ref, knew_ref, wuv_ref, lat_hbm, rope_hbm, o_ref,
                  latbuf, ropebuf, kcat, sem, *, layer, n_pages):
    b = pl.program_id(0)
    nb = pl.num_programs(0)

    def copies(seq, slot):
        out = []
        for pg in range(n_pages):
            page = pt_ref[seq, pg]
            rows = pl.ds(pg * PAGE_SIZE, PAGE_SIZE)
            out.append(pltpu.make_async_copy(lat_hbm.at[layer, page], latbuf.at[slot, rows, :], sem.at[0, slot]))
            out.append(pltpu.make_async_copy(rope_hbm.at[layer, page], ropebuf.at[slot, rows, :], sem.at[1, slot]))
        return out

    @pl.when(b == 0)
    def _():
        kcat[:, MLA_KV_LORA + MLA_ROPE:] = jnp.zeros((kcat.shape[0], QK_W - MLA_KV_LORA - MLA_ROPE), BF16)
        for cp in copies(0, 0):
            cp.start()

    slot = b % 2

    @pl.when(b + 1 < nb)
    def _():
        for cp in copies(b + 1, 1 - slot):
            cp.start()

    for cp in copies(b, slot):
        cp.wait()

    kcat[:, 0:MLA_KV_LORA] = latbuf[slot].astype(BF16)
    kcat[:, MLA_KV_LORA:MLA_KV_LORA + MLA_ROPE] = ropebuf[slot].astype(BF16)

    q = q_ref[0]
    knew = knew_ref[0].astype(F32)
    s_past = _dot_nt(q, kcat[...])
    s_new = jnp.sum(q.astype(F32) * knew, axis=1, keepdims=True)
    mx = jnp.maximum(jnp.max(s_past, axis=1, keepdims=True), s_new)
    p_past = jnp.exp(s_past - mx)
    p_new = jnp.exp(s_new - mx)
    denom = jnp.sum(p_past, axis=1, keepdims=True) + p_new
    acc = _dot(p_past.astype(BF16), kcat[:, 0:MLA_KV_LORA])
    acc = acc + p_new.astype(BF16).astype(F32) * knew[:, 0:MLA_KV_LORA]
    o_lat = (acc / denom).astype(BF16)
    row = lax.broadcasted_iota(jnp.int32, (Q_ROWS, MLA_V), 0)
    out = jnp.zeros((Q_ROWS, MLA_V), F32)
    for h in range(MLA_HEADS):
        out = out + jnp.where(row == h, _dot(o_lat, wuv_ref[h]), 0.0)
    o_ref[0] = out


def mla_paged(page_table, q8, knew, wuv, cache_lat, cache_rope, layer):
    b, n_pages = page_table.shape
    n_past = n_pages * PAGE_SIZE
    kern = functools.partial(_paged_kernel, layer=layer, n_pages=n_pages)
    grid_spec = pltpu.PrefetchScalarGridSpec(
        num_scalar_prefetch=1,
        grid=(b,),
        in_specs=[pl.BlockSpec((1, Q_ROWS, QK_W), lambda i, pt: (i, 0, 0)),
                  pl.BlockSpec((1, 1, QK_W), lambda i, pt: (i, 0, 0)),
                  pl.BlockSpec((MLA_HEADS, MLA_KV_LORA, MLA_V), lambda i, pt: (0, 0, 0)),
                  pl.BlockSpec(memory_space=pl.ANY),
                  pl.BlockSpec(memory_space=pl.ANY)],
        out_specs=pl.BlockSpec((1, Q_ROWS, MLA_V), lambda i, pt: (i, 0, 0)),
        scratch_shapes=[pltpu.VMEM((2, n_past, MLA_KV_LORA), F32),
                        pltpu.VMEM((2, n_past, MLA_ROPE), F32),
                        pltpu.VMEM((n_past, QK_W), BF16),
                        pltpu.SemaphoreType.DMA((2, 2))],
    )
    return pl.pallas_call(
        kern,
        grid_spec=grid_spec,
        out_shape=jax.ShapeDtypeStruct((b, Q_ROWS, MLA_V), F32),
        compiler_params=_params("arbitrary"),
        name="mla_paged",
    )(page_table, q8, knew, wuv, cache_lat, cache_rope)


def _mix_kernel(b0_ref, b1_ref, b2_ref, b3_ref, g0_ref, g1_ref, g2_ref, g3_ref, wb_ref, o_ref):
    acc = None
    for br_ref, g_ref, i in ((b0_ref, g0_ref, 0), (b1_ref, g1_ref, 1), (b2_ref, g2_ref, 2), (b3_ref, g3_ref, 3)):
        term = _sigmoid(g_ref[...]) * _dot(br_ref[...], wb_ref[i])
        acc = term if acc is None else acc + term
    o_ref[...] = acc.astype(BF16)


def branch_mix(branches, p, wb, tm, tn):
    m = p.shape[0]
    nj = D_MODEL // tn
    br_spec = pl.BlockSpec((tm, BRANCH_W), lambda i, j: (i, 0))
    gate_specs = [pl.BlockSpec((tm, tn), functools.partial(lambda i, j, g: (i, g * nj + j), g=g))
                  for g in range(N_BRANCH)]
    return pl.pallas_call(
        _mix_kernel,
        grid=(m // tm, nj),
        in_specs=[br_spec] * 4 + gate_specs + [pl.BlockSpec((N_BRANCH, BRANCH_W, tn), lambda i, j: (0, 0, j))],
        out_specs=pl.BlockSpec((tm, tn), lambda i, j: (i, j)),
        out_shape=jax.ShapeDtypeStruct((m, D_MODEL), BF16),
        compiler_params=_params("parallel", "parallel"),
        name="branch_mix",
    )(*branches, p, p, p, p, wb)


def _proj_norm_res_kernel(a_ref, w_ref, x_ref, g_ref, o_ref):
    o_ref[...] = x_ref[...] + _rms(_dot(a_ref[...], w_ref[...]), g_ref[...])


def proj_norm_res(a, w, x, g, tm):
    m, k = a.shape
    n = w.shape[1]
    return pl.pallas_call(
        _proj_norm_res_kernel,
        grid=(m // tm,),
        in_specs=[pl.BlockSpec((tm, k), lambda i: (i, 0)),
                  pl.BlockSpec((k, n), lambda i: (0, 0)),
                  pl.BlockSpec((tm, n), lambda i: (i, 0)),
                  pl.BlockSpec((1, n), lambda i: (0, 0))],
        out_specs=pl.BlockSpec((tm, n), lambda i: (i, 0)),
        out_shape=jax.ShapeDtypeStruct((m, n), F32),
        compiler_params=_params("parallel"),
        name="proj_norm_res",
    )(a, w, x, g)


FF_TN = 512
FF_NJ = D_FF // FF_TN


def _ffn_up_prompt_kernel(x_ref, g_ref, wa_ref, wg_ref, cwa_ref, cwg_ref, cba_ref, cbg_ref,
                          act_ref, sta_ref, stg_ref, xn_ref, carry_ref, *, tiles_per_seq):
    i = pl.program_id(0)
    j = pl.program_id(1)
    tm = x_ref.shape[0]

    @pl.when(j == 0)
    def _():
        xn_ref[...] = _rms(x_ref[...], g_ref[...]).astype(BF16)

    @pl.when(i % tiles_per_seq == 0)
    def _():
        carry_ref[j] = jnp.zeros(carry_ref.shape[1:], F32)

    row = lax.broadcasted_iota(jnp.int32, (tm, 1), 0)
    xn = xn_ref[...]

    def half(w_ref, cw_ref, cb_ref, st_ref, slot):
        up = _dot(xn, w_ref[...])
        prev = carry_ref[j, slot]
        p0 = prev[0:1, :]
        p1 = prev[1:2, :]
        m1 = jnp.where(row == 0, p1, pltpu.roll(up, 1, axis=0))
        m2 = jnp.where(row == 0, p0, jnp.where(row == 1, p1, pltpu.roll(up, 2, axis=0)))
        cw = cw_ref[...]
        tail = up[tm - 2:tm, :]
        carry_ref[j, slot] = tail
        st_ref[0] = tail
        return cb_ref[...] + m2 * cw[0:1, :] + m1 * cw[1:2, :] + up * cw[2:3, :]

    a = half(wa_ref, cwa_ref, cba_ref, sta_ref, 0)
    g = half(wg_ref, cwg_ref, cbg_ref, stg_ref, 1)
    act_ref[...] = (a * _gelu(g)).astype(BF16)


def ffn_up_prompt(x, g, wup, cw, cb, n_seq, seq, tm):
    m = x.shape[0]
    tps = seq // tm
    kern = functools.partial(_ffn_up_prompt_kernel, tiles_per_seq=tps)
    return pl.pallas_call(
        kern,
        grid=(m // tm, FF_NJ),
        in_specs=[pl.BlockSpec((tm, D_MODEL), lambda i, j: (i, 0)),
                  pl.BlockSpec((1, D_MODEL), lambda i, j: (0, 0)),
                  pl.BlockSpec((D_MODEL, FF_TN), lambda i, j: (0, j)),
                  pl.BlockSpec((D_MODEL, FF_TN), lambda i, j: (0, FF_NJ + j)),
                  pl.BlockSpec((CONV_W, FF_TN), lambda i, j: (0, j)),
                  pl.BlockSpec((CONV_W, FF_TN), lambda i, j: (0, FF_NJ + j)),
                  pl.BlockSpec((1, FF_TN), lambda i, j: (0, j)),
                  pl.BlockSpec((1, FF_TN), lambda i, j: (0, FF_NJ + j))],
        out_specs=[pl.BlockSpec((tm, FF_TN), lambda i, j: (i, j)),
                   pl.BlockSpec((1, CONV_W - 1, FF_TN), lambda i, j: (i, 0, j)),
                   pl.BlockSpec((1, CONV_W - 1, FF_TN), lambda i, j: (i, 0, j))],
        out_shape=[jax.ShapeDtypeStruct((m, D_FF), BF16),
                   jax.ShapeDtypeStruct((m // tm, CONV_W - 1, D_FF), F32),
                   jax.ShapeDtypeStruct((m // tm, CONV_W - 1, D_FF), F32)],
        scratch_shapes=[pltpu.VMEM((tm, D_MODEL), BF16),
                        pltpu.VMEM((FF_NJ, 2, CONV_W - 1, FF_TN), F32)],
        compiler_params=_params("arbitrary", "arbitrary"),
        name="ffn_up_prompt",
    )(x, g, wup, wup, cw, cw, cb, cb)


def _ffn_up_decode_kernel(x_ref, g_ref, wa_ref, wg_ref, cwa_ref, cwg_ref, cba_ref, cbg_ref,
                          b0a_ref, b1a_ref, b0g_ref, b1g_ref, act_ref, upa_ref, upg_ref, xn_ref):
    @pl.when(pl.program_id(0) == 0)
    def _():
        xn_ref[...] = _rms(x_ref[...], g_ref[...]).astype(BF16)

    xn = xn_ref[...]

    def half(w_ref, cw_ref, cb_ref, b0_ref, b1_ref, up_ref):
        up = _dot(xn, w_ref[...])
        up_ref[...] = up
        cw = cw_ref[...]
        return cb_ref[...] + b0_ref[...] * cw[0:1, :] + b1_ref[...] * cw[1:2, :] + up * cw[2:3, :]

    a = half(wa_ref, cwa_ref, cba_ref, b0a_ref, b1a_ref, upa_ref)
    g = half(wg_ref, cwg_ref, cbg_ref, b0g_ref, b1g_ref, upg_ref)
    act_ref[...] = (a * _gelu(g)).astype(BF16)


def ffn_up_decode(x, g, wup, cw, cb, buf0, buf1):
    b = x.shape[0]
    col_a = lambda j: (0, j)
    col_g = lambda j: (0, FF_NJ + j)
    return pl.pallas_call(
        _ffn_up_decode_kernel,
        grid=(FF_NJ,),
        in_specs=[pl.BlockSpec((b, D_MODEL), lambda j: (0, 0)),
                  pl.BlockSpec((1, D_MODEL), lambda j: (0, 0)),
                  pl.BlockSpec((D_MODEL, FF_TN), col_a), pl.BlockSpec((D_MODEL, FF_TN), col_g),
                  pl.BlockSpec((CONV_W, FF_TN), col_a), pl.BlockSpec((CONV_W, FF_TN), col_g),
                  pl.BlockSpec((1, FF_TN), col_a), pl.BlockSpec((1, FF_TN), col_g),
                  pl.BlockSpec((b, FF_TN), col_a), pl.BlockSpec((b, FF_TN), col_a),
                  pl.BlockSpec((b, FF_TN), col_g), pl.BlockSpec((b, FF_TN), col_g)],
        out_specs=[pl.BlockSpec((b, FF_TN), col_a),
                   pl.BlockSpec((b, FF_TN), col_a), pl.BlockSpec((b, FF_TN), col_a)],
        out_shape=[jax.ShapeDtypeStruct((b, D_FF), BF16),
                   jax.ShapeDtypeStruct((b, D_FF), F32), jax.ShapeDtypeStruct((b, D_FF), F32)],
        scratch_shapes=[pltpu.VMEM((b, D_MODEL), BF16)],
        compiler_params=_params("arbitrary"),
        name="ffn_up_decode",
    )(x, g, wup, wup, cw, cw, cb, cb, buf0, buf1, buf0, buf1)


def _ffn_down_kernel(a_ref, w_ref, x_ref, g_ref, o_ref, acc_ref):
    k = pl.program_id(1)

    @pl.when(k == 0)
    def _():
        acc_ref[...] = jnp.zeros_like(acc_ref)

    acc_ref[...] += _dot(a_ref[...], w_ref[...])

    @pl.when(k == pl.num_programs(1) - 1)
    def _():
        o_ref[...] = x_ref[...] + _rms(acc_ref[...], g_ref[...])


def ffn_down(act, w, x, g, tm, tk):
    m, kk = act.shape
    n = w.shape[1]
    return pl.pallas_call(
        _ffn_down_kernel,
        grid=(m // tm, kk // tk),
        in_specs=[pl.BlockSpec((tm, tk), lambda i, k: (i, k)),
                  pl.BlockSpec((tk, n), lambda i, k: (k, 0)),
                  pl.BlockSpec((tm, n), lambda i, k: (i, 0)),
                  pl.BlockSpec((1, n), lambda i, k: (0, 0))],
        out_specs=pl.BlockSpec((tm, n), lambda i, k: (i, 0)),
        out_shape=jax.ShapeDtypeStruct((m, n), F32),
        scratch_shapes=[pltpu.VMEM((tm, n), F32)],
        compiler_params=_params("parallel", "arbitrary"),
        name="ffn_down",
    )(act, w, x, g)


def _swap_halves_cols(w):
    half = w.shape[-1] // 2
    return jnp.concatenate([-w[..., half:], w[..., :half]], axis=-1)


def _prep_w_in(w):
    o = 0
    parts = {}
    for name, n in (("gate", N_BRANCH * D_MODEL), ("gq", 256), ("gk", 256), ("gv", 512), ("gog", 512),
                    ("alr", GLA_LOWRANK), ("su", 512), ("sz", 1024), ("cq", MLA_Q_LORA), ("ckv", MLA_KV_LORA),
                    ("kr", MLA_ROPE)):
        parts[name] = w[:, o:o + n]
        o += n
    z = lambda n: jnp.zeros((w.shape[0], n), w.dtype)
    cols = [parts["gate"], parts["gq"], parts["gk"], parts["gv"], parts["gog"], parts["su"], parts["sz"],
            parts["ckv"], parts["kr"], z(LANES - MLA_ROPE), parts["cq"],
            _swap_halves_cols(parts["kr"]), z(LANES - MLA_ROPE),
            parts["alr"], z(LANES - GLA_LOWRANK), z(N_IN_PAD - COL_ALR - LANES)]
    return jnp.concatenate(cols, axis=1).astype(BF16)


def _prep_w_uq(w):
    w = w.reshape(MLA_Q_LORA, MLA_HEADS, MLA_NOPE + MLA_ROPE)
    z = jnp.zeros((MLA_Q_LORA, MLA_HEADS, LANES - MLA_ROPE), w.dtype)
    main = jnp.concatenate([w, z], axis=2).reshape(MLA_Q_LORA, MLA_HEADS * QK_W)
    sw = jnp.concatenate([_swap_halves_cols(w[:, :, MLA_NOPE:]), z], axis=2).reshape(MLA_Q_LORA, MLA_HEADS * LANES)
    return jnp.concatenate([main, sw], axis=1).astype(BF16)


def _prep_ssm(a_re, a_im, b_re, b_im, c_re, c_im, d, log_dt, w_glu, b_glu, n_pow):
    g, n, cg = SSM_GROUPS, SSM_STATE, SSM_GROUP_CH
    dt = jnp.exp(log_dt)[:, None]
    mag = jnp.exp(dt * a_re)
    abr, abi = mag * jnp.cos(dt * a_im), mag * jnp.sin(dt * a_im)
    den = a_re * a_re + a_im * a_im
    fr = ((abr - 1.0) * a_re + abi * a_im) / den
    fi = (abi * a_re - (abr - 1.0) * a_im) / den
    bbr = fr[:, :, None] * b_re - fi[:, :, None] * b_im
    bbi = fr[:, :, None] * b_im + fi[:, :, None] * b_re
    gps = g // SSM_SB

    def blockdiag_in(bm):
        bm = bm.reshape(SSM_SB, gps, n, cg)
        eye = jnp.eye(gps, dtype=bm.dtype)
        out = jnp.einsum("sgnc,gh->sgchn", bm, eye)
        return out.reshape(SSM_SB, gps * cg, gps * n).astype(BF16)

    def blockdiag_out(cm):
        cm = cm.reshape(SSM_SB, gps, cg, n)
        eye = jnp.eye(gps, dtype=cm.dtype)
        out = jnp.einsum("sgcn,gh->sgnhc", cm, eye)
        return out.reshape(SSM_SB, gps * n, gps * cg).astype(BF16)

    ar = abr.reshape(1, SSM_LANES)
    ai = abi.reshape(1, SSM_LANES)
    pr, pi_ = ar, ai
    pwr, pwi = [pr], [pi_]
    for _ in range(n_pow - 1):
        pr, pi_ = pr * ar - pi_ * ai, pr * ai + pi_ * ar
        pwr.append(pr)
        pwi.append(pi_)
    return dict(bre=blockdiag_in(bbr), bim=blockdiag_in(bbi), ar=ar, ai=ai,
                pwr=jnp.broadcast_to(jnp.stack(pwr), (n_pow, SUBLANES, SSM_LANES)),
                pwi=jnp.broadcast_to(jnp.stack(pwi), (n_pow, SUBLANES, SSM_LANES)),
                cre=blockdiag_out(c_re), cim=blockdiag_out(c_im),
                d=d.reshape(1, -1), wglu=w_glu.astype(BF16), bglu=b_glu.reshape(1, -1))


def _rope_tables(pos):
    half = MLA_ROPE // 2
    inv = ROPE_THETA ** (-jnp.arange(half, dtype=F32) / half)
    ang = pos.astype(F32)[:, None] * inv[None, :]
    z = jnp.zeros((pos.shape[0], LANES - MLA_ROPE), F32)
    cos = jnp.concatenate([jnp.cos(ang), jnp.cos(ang), z], axis=1)
    sin = jnp.concatenate([jnp.sin(ang), jnp.sin(ang), z], axis=1)
    return cos, sin


SSM_T = 256
PROMPT_TM = 1024


def _layer_prompt(x, w, n_seq, seq, cos, sin):
    p = norm_matmul(x, w["g_mix_pre"], w["w_in"], PROMPT_TM, 1024)
    o_a, gla_t = gla_prompt(p, w["wup"], w["ba"], w["gn"], n_seq, seq, 512)
    o_b, ssm_st = ssm_prompt(p, w["ssm"], n_seq, seq, SSM_T)
    o_c, sgu_v = sgu_prompt(p, w["lng"], w["lnb"], w["ws"], w["bsb"], n_seq, seq, 512)
    q, kcat, lat, rope = mla_prep(p, cos, sin, w["gqn"], w["wuq"], w["wuk"], w["gkvn"], 512, seq // 512)
    o_d = mla_flash(q, kcat, w["wuv"], n_seq, seq, 256, 512)
    mix = branch_mix((o_a, o_b, o_c, o_d), p, w["wb"], PROMPT_TM, 512)
    x = proj_norm_res(mix, w["w_out"], x, w["g_mix_post"], 512)
    act, st_a, st_g = ffn_up_prompt(x, w["g_ffn_pre"], w["w_up"], w["cw"], w["cb"], n_seq, seq, PROMPT_TM)
    x = ffn_down(act, w["w_down"], x, w["g_ffn_post"], 512, 1408)
    gla_state = gla_t.reshape(n_seq, GLA_HEADS, GLA_DV, GLA_HEADS, GLA_DK)
    gla_state = jnp.stack([gla_state[:, h, :, h, :] for h in range(GLA_HEADS)], axis=1)
    gla_state = jnp.swapaxes(gla_state, 2, 3)
    ssm_state = jnp.stack([ssm_st[:, 0].reshape(n_seq, SSM_GROUPS, SSM_STATE),
                           ssm_st[:, 1].reshape(n_seq, SSM_GROUPS, SSM_STATE)], axis=-1)
    tps = seq // PROMPT_TM
    conv_state = jnp.concatenate([st_a[tps - 1::tps], st_g[tps - 1::tps]], axis=-1)
    return x, (lat.reshape(n_seq, seq, -1), rope.reshape(n_seq, seq, -1), gla_state, ssm_state, conv_state, sgu_v)


def _layer_decode(x, w, layer, cos, sin, page_table, cache_lat, cache_rope, gla_s0, ssm_x0, conv_buf):
    b = x.shape[0]
    p = norm_matmul(x, w["g_mix_pre"], w["w_in"], b, 1024)
    q_h = p[:, COL_GQ:COL_GQ + 256].reshape(b, GLA_HEADS, GLA_DK).transpose(1, 0, 2)
    k_h = p[:, COL_GK:COL_GK + 256].reshape(b, GLA_HEADS, GLA_DK).transpose(1, 0, 2)
    o_a, gla_state = gla_decode(p, q_h, k_h, w["wup_h"], w["ba_h"], w["gn"], gla_s0)
    x0r = ssm_x0[..., 0].reshape(b, SSM_LANES)
    x0i = ssm_x0[..., 1].reshape(b, SSM_LANES)
    o_b, hr, hi = ssm_decode(p, x0r, x0i, w["ssm"])
    o_c, sgu_v = sgu_decode(p, w["lng"], w["lnb"], w["ws0"], w["bs0"])
    q, kcat, lat, rope = mla_prep(p, cos, sin, w["gqn"], w["wuq"], w["wuk"], w["gkvn"], b, 1)
    q8 = jnp.concatenate([q.transpose(1, 0, 2), jnp.zeros((b, Q_ROWS - MLA_HEADS, QK_W), BF16)], axis=1)
    o8 = mla_paged(page_table, q8, kcat.reshape(b, 1, QK_W), w["wuv"], cache_lat, cache_rope, layer)
    o_d = o8[:, :MLA_HEADS, :].reshape(b, MLA_HEADS * MLA_V).astype(BF16)
    mix = branch_mix((o_a, o_b, o_c, o_d), p, w["wb"], b, 512)
    x = proj_norm_res(mix, w["w_out"], x, w["g_mix_post"], b)
    act, up_a, up_g = ffn_up_decode(x, w["g_ffn_pre"], w["w_up"], w["cw"], w["cb"], conv_buf[:, 0], conv_buf[:, 1])
    x = ffn_down(act, w["w_down"], x, w["g_ffn_post"], b, 1408)
    ssm_state = jnp.stack([hr.reshape(b, SSM_GROUPS, SSM_STATE), hi.reshape(b, SSM_GROUPS, SSM_STATE)], axis=-1)
    conv_state = jnp.stack([conv_buf[:, 1], jnp.concatenate([up_a, up_g], axis=-1)], axis=1)
    return x, (lat.reshape(b, 1, -1), rope.reshape(b, 1, -1), gla_state, ssm_state, conv_state,
               sgu_v.reshape(b, 1, -1))


def kernel(x_prompt, x_sample, cache_mla_latent, cache_mla_rope, page_table, state_gla, state_ssm, state_ffn_conv, g_mix_pre, g_mix_post, g_ffn_pre, g_ffn_post, w_in, w_gla_a_up, b_gla_a, g_gla_norm, ssm_a_re, ssm_a_im, ssm_b_re, ssm_b_im, ssm_c_re, ssm_c_im, ssm_d, ssm_log_dt, w_ssm_glu, b_ssm_glu, sgu_ln_g, sgu_ln_b, w_sgu_s, b_sgu_s, g_mla_qn, w_mla_uq, g_mla_kvn, w_mla_uk, w_mla_uv, w_branch, w_out, w_ffn_up, ffn_conv_w, ffn_conv_b, w_ffn_down):
    n_seq, seq, _ = x_prompt.shape
    n_dec = x_sample.shape[0]
    depth = w_in.shape[0]
    past_len = page_table.shape[1] * PAGE_SIZE
    cos_p, sin_p = _rope_tables(jnp.arange(seq, dtype=jnp.int32))
    cos_s, sin_s = _rope_tables(jnp.full((n_dec,), past_len, dtype=jnp.int32))

    yp = x_prompt.reshape(n_seq * seq, D_MODEL)
    ys = x_sample.reshape(n_dec, D_MODEL)
    st_p, st_s = [], []
    for l in range(depth):
        row = lambda a: a[l].reshape(1, -1)
        w = dict(
            g_mix_pre=row(g_mix_pre), g_mix_post=row(g_mix_post), g_ffn_pre=row(g_ffn_pre), g_ffn_post=row(g_ffn_post),
            w_in=_prep_w_in(w_in[l]),
            wup=w_gla_a_up[l].astype(BF16), ba=row(b_gla_a), gn=row(g_gla_norm),
            wup_h=w_gla_a_up[l].reshape(GLA_LOWRANK, GLA_HEADS, GLA_DK).transpose(1, 0, 2).astype(BF16),
            ba_h=b_gla_a[l].reshape(GLA_HEADS, 1, GLA_DK),
            ssm=_prep_ssm(ssm_a_re[l], ssm_a_im[l], ssm_b_re[l], ssm_b_im[l], ssm_c_re[l], ssm_c_im[l], ssm_d[l],
                          ssm_log_dt[l], w_ssm_glu[l], b_ssm_glu[l], SSM_T // SUBLANES),
            lng=row(sgu_ln_g), lnb=row(sgu_ln_b), ws=w_sgu_s[l],
            bsb=jnp.broadcast_to(b_sgu_s[l][:, :, None], (SGU_HEADS, SGU_CHUNK, SGU_HEAD_W)),
            ws0=jnp.repeat(w_sgu_s[l][:, 0, 0], SGU_HEAD_W).reshape(1, -1),
            bs0=jnp.repeat(b_sgu_s[l][:, 0], SGU_HEAD_W).reshape(1, -1),
            gqn=row(g_mla_qn), wuq=_prep_w_uq(w_mla_uq[l]), gkvn=row(g_mla_kvn),
            wuk=jnp.transpose(w_mla_uk[l], (1, 2, 0)).astype(BF16),
            wuv=jnp.transpose(w_mla_uv[l], (1, 0, 2)).astype(BF16),
            wb=w_branch[l].astype(BF16), w_out=w_out[l].astype(BF16),
            w_up=w_ffn_up[l].astype(BF16), cw=ffn_conv_w[l], cb=row(ffn_conv_b), w_down=w_ffn_down[l].astype(BF16),
        )
        yp, sp = _layer_prompt(yp, w, n_seq, seq, cos_p, sin_p)
        ys, ss = _layer_decode(ys, w, l, cos_s, sin_s, page_table, cache_mla_latent, cache_mla_rope,
                               state_gla[l], state_ssm[l], state_ffn_conv[l])
        st_p.append(sp)
        st_s.append(ss)
    lat_p, rope_p, gla_p, ssm_p, conv_p, sgu_p = [jnp.stack(t) for t in zip(*st_p)]
    lat_s, rope_s, gla_s, ssm_s, conv_s, sgu_s = [jnp.stack(t) for t in zip(*st_s)]
    return (yp.reshape(n_seq, seq, D_MODEL), ys.reshape(n_dec, 1, D_MODEL), lat_p, lat_s, rope_p, rope_s,
            gla_p, gla_s, ssm_p, ssm_s, conv_p, conv_s, sgu_p, sgu_s)
```
